```python
import math
import jax, jax.numpy as jnp
from jax import lax
import numpy as np

D_MODEL = 2048
BATCH = 16
SEQ = 256
DEPTH = 2
DEC_BATCH = 2
DEC_SEQ = 2048
PAST_LEN = 256

GRID_W = 64
HEAD_DIM = 128
N_BRANCH = 4
HEADS_PER_MIXER = D_MODEL // (N_BRANCH * HEAD_DIM)
NA_HEADS = HEADS_PER_MIXER
DIFF_HEADS = HEADS_PER_MIXER
DIFF_HALF = HEAD_DIM // 2
WIN_Q_HEADS = HEADS_PER_MIXER
WIN_KV_HEADS = HEADS_PER_MIXER // 2
GQA_Q_HEADS = HEADS_PER_MIXER
GQA_KV_HEADS = HEADS_PER_MIXER // 2
BRANCH_WIDTH = HEADS_PER_MIXER * HEAD_DIM
NA_WIN_ROWS = 8
NA_WIN_COLS = 16
WINDOW = 128
Q_BLOCK = 128
ROPE_BASE = 10000.0
N_EXPERTS = 16
EXPERT_FF = D_MODEL
CAPACITY_FACTOR = 2
EPS = 1e-6
PROJ_SIZES = (NA_HEADS * HEAD_DIM, NA_HEADS * HEAD_DIM, NA_HEADS * HEAD_DIM,
              DIFF_HEADS * HEAD_DIM, DIFF_HEADS * HEAD_DIM, DIFF_HEADS * HEAD_DIM,
              WIN_Q_HEADS * HEAD_DIM, WIN_KV_HEADS * HEAD_DIM, WIN_KV_HEADS * HEAD_DIM,
              GQA_Q_HEADS * HEAD_DIM, GQA_KV_HEADS * HEAD_DIM, GQA_KV_HEADS * HEAD_DIM,
              N_BRANCH * D_MODEL)
PROJ_OFFSETS = tuple(int(o) for o in np.cumsum(PROJ_SIZES)[:-1])
PROJ_WIDTH = int(sum(PROJ_SIZES))

kernel_name = "hybrid_ctxprefix_diffusion_step"


def rms_norm(x, g):
    xf = x.astype(jnp.float32)
    y = xf * lax.rsqrt(jnp.mean(xf * xf, axis=-1, keepdims=True) + EPS)
    return (y * g.astype(jnp.float32)).astype(x.dtype)


def heads(t, n):
    B, L, _ = t.shape
    return t.reshape(B, L, n, HEAD_DIM).transpose(0, 2, 1, 3)


def merge_heads(t):
    B, n, L, d = t.shape
    return t.transpose(0, 2, 1, 3).reshape(B, L, n * d)


def group_q(t, n_kv):
    B, H, L, d = t.shape
    return t.reshape(B, n_kv, H // n_kv, L, d)


def ungroup(t):
    B, Hkv, G, L, d = t.shape
    return t.reshape(B, Hkv * G, L, d)


def split_halves(t):
    return jnp.stack(jnp.split(t, 2, axis=-1), axis=2)


def axial_rope(x):
    L, dim = x.shape[-2], x.shape[-1]
    nf = dim // 4
    t = jnp.arange(L)
    row = (t // GRID_W).astype(jnp.float32)
    col = (t % GRID_W).astype(jnp.float32)
    inv = ROPE_BASE ** (-jnp.arange(nf, dtype=jnp.float32) / nf)
    ar, ac = row[:, None] * inv, col[:, None] * inv
    ang = jnp.concatenate([ar, ar, ac, ac], axis=-1)
    xf = x.astype(jnp.float32)
    x1, x2, x3, x4 = jnp.split(xf, 4, axis=-1)
    rot = jnp.concatenate([-x2, x1, -x4, x3], axis=-1)
    return (xf * jnp.cos(ang) + rot * jnp.sin(ang)).astype(x.dtype)


def softmax_sink(s, sink):
    m = jnp.maximum(jnp.max(s, axis=-1, keepdims=True), sink)
    e = jnp.exp(s - m)
    return e / (jnp.sum(e, axis=-1, keepdims=True) + jnp.exp(sink - m))


def dense_attention(q, k, v, sink=None):
    B, Hkv, G, L, d = q.shape
    nb = L // Q_BLOCK
    qb = jnp.moveaxis(q.reshape(B, Hkv, G, nb, Q_BLOCK, d), 3, 0)
    scale = d ** -0.5

    def block(qi):
        s = jnp.einsum('bhgqd,bhkd->bhgqk', qi, k).astype(jnp.float32) * scale
        if sink is None:
            p = jax.nn.softmax(s, axis=-1)
        else:
            p = softmax_sink(s, sink.astype(jnp.float32)[None, :, :, None, None])
        return jnp.einsum('bhgqk,bhkd->bhgqd', p.astype(v.dtype), v)

    o = lax.map(block, qb)
    return jnp.moveaxis(o, 0, 3).reshape(B, Hkv, G, L, d)


def diff_attention(q, k, v, lam):
    B, H, _, L, dh = q.shape
    nb = L // Q_BLOCK
    qb = jnp.moveaxis(q.reshape(B, H, 2, nb, Q_BLOCK, dh), 3, 0)
    scale = dh ** -0.5

    def block(qi):
        s = jnp.einsum('bhiqd,bhikd->bhiqk', qi, k).astype(jnp.float32) * scale
        p = jax.nn.softmax(s, axis=-1)
        a = p[:, :, 0] - lam * p[:, :, 1]
        return jnp.einsum('bhqk,bhkd->bhqd', a.astype(v.dtype), v)

    o = lax.map(block, qb)
    return jnp.moveaxis(o, 0, 2).reshape(B, H, L, 2 * dh)


def diff_branch(q12, k12, v, lam_p, subln_g, layer):
    lam_init = 0.8 - 0.6 * math.exp(-0.3 * layer)
    lp = lam_p.astype(jnp.float32)
    lam = jnp.exp(jnp.sum(lp[0] * lp[1])) - jnp.exp(jnp.sum(lp[2] * lp[3])) + lam_init
    o = diff_attention(q12, k12, v, lam)
    return rms_norm(o, subln_g) * (1.0 - lam_init)


def neighbourhood_attention(q, k, v, kc, vc, rpb):
    B, H, L, d = q.shape
    rows = L // GRID_W
    wr = min(NA_WIN_ROWS, rows)
    r = jnp.arange(rows)
    rs = jnp.clip(r - wr // 2, 0, rows - wr)
    row_idx = rs[:, None] + jnp.arange(wr)[None, :]
    c = jnp.arange(GRID_W)
    cs = jnp.clip(c - NA_WIN_COLS // 2, 0, GRID_W - NA_WIN_COLS)
    col_valid = (c[None, :] >= cs[:, None]) & (c[None, :] < cs[:, None] + NA_WIN_COLS)
    qg = q.reshape(B, H, rows, GRID_W, d)
    kg = k.reshape(B, H, rows, GRID_W, d)[:, :, row_idx]
    vg = v.reshape(B, H, rows, GRID_W, d)[:, :, row_idx]
    scale = d ** -0.5
    s_nb = jnp.einsum('bhrqd,bhrjcd->bhrqjc', qg, kg).astype(jnp.float32) * scale
    dr = row_idx - r[:, None] + NA_WIN_ROWS - 1
    dc = jnp.clip(c[None, :] - c[:, None], -(NA_WIN_COLS - 1), NA_WIN_COLS - 1) + NA_WIN_COLS - 1
    bias = rpb.astype(jnp.float32)[:, dr[:, None, :, None], dc[None, :, None, :]]
    s_nb = jnp.where(col_valid[:, None, :], s_nb + bias[None], -jnp.inf)
    s_ctx = jnp.einsum('bhrqd,bhpd->bhrqp', qg, kc).astype(jnp.float32) * scale
    n_nb = wr * GRID_W
    s = jnp.concatenate([s_nb.reshape(B, H, rows, GRID_W, n_nb), s_ctx], axis=-1)
    p = jax.nn.softmax(s, axis=-1).astype(v.dtype)
    p_nb = p[..., :n_nb].reshape(B, H, rows, GRID_W, wr, GRID_W)
    o = (jnp.einsum('bhrqjc,bhrjcd->bhrqd', p_nb, vg)
         + jnp.einsum('bhrqp,bhpd->bhrqd', p[..., n_nb:], vc))
    return o.reshape(B, H, L, d)


def window_attention(q, k, v, kc, vc, sink):
    B, Hkv, G, L, d = q.shape
    nb = L // WINDOW

    def bands(t):
        tp = jnp.pad(t, ((0, 0), (0, 0), (WINDOW, WINDOW), (0, 0))).reshape(B, Hkv, nb + 2, WINDOW, d)
        return jnp.concatenate([tp[:, :, :-2], tp[:, :, 1:-1], tp[:, :, 2:]], axis=3)

    kb, vb = bands(k), bands(v)
    qb = q.reshape(B, Hkv, G, nb, WINDOW, d)
    scale = d ** -0.5
    s_loc = jnp.einsum('bhgnqd,bhnkd->bhgnqk', qb, kb).astype(jnp.float32) * scale
    n = jnp.arange(nb)[:, None, None]
    i = jnp.arange(WINDOW)[None, :, None]
    j = jnp.arange(3 * WINDOW)[None, None, :]
    qpos = n * WINDOW + i
    kpos = (n - 1) * WINDOW + j
    valid = (jnp.abs(qpos - kpos) <= WINDOW) & (kpos >= 0) & (kpos < L)
    s_ctx = jnp.einsum('bhgnqd,bhpd->bhgnqp', qb, kc).astype(jnp.float32) * scale
    s = jnp.concatenate([jnp.where(valid, s_loc, -jnp.inf), s_ctx], axis=-1)
    p = softmax_sink(s, sink.astype(jnp.float32)[None, :, :, None, None, None]).astype(v.dtype)
    o = (jnp.einsum('bhgnqk,bhnkd->bhgnqd', p[..., :3 * WINDOW], vb)
         + jnp.einsum('bhgnqp,bhpd->bhgnqd', p[..., 3 * WINDOW:], vc))
    return o.reshape(B, Hkv, G, L, d)


def merge_branches(outs, gates, w_branch, w_out):
    B, L, _ = gates.shape
    o = jnp.stack([merge_heads(t) for t in outs], axis=2)
    g = jax.nn.sigmoid(gates.reshape(B, L, N_BRANCH, D_MODEL))
    y = jnp.einsum('blic,ice->blie', o, w_branch)
    return jnp.einsum('ble,ef->blf', jnp.sum(g * y, axis=2), w_out)


def project(h, w_in):
    return jnp.split(jnp.einsum('bld,de->ble', h, w_in), PROJ_OFFSETS, axis=-1)


def context_mixer(h, w_in, lam_p, subln_g, sink, qn_g, kn_g, w_branch, w_out, layer):
    qa, ka, va, qb, kb, vb, qc, kc, vc, qd, kd, vd, gates = project(h, w_in)
    qa, ka, va = heads(qa, NA_HEADS), heads(ka, NA_HEADS), heads(va, NA_HEADS)
    o_a = dense_attention(qa[:, :, None], ka, va)[:, :, 0]
    kb, vb = heads(kb, DIFF_HEADS), heads(vb, DIFF_HEADS)
    o_b = diff_branch(split_halves(heads(qb, DIFF_HEADS)), split_halves(kb), vb, lam_p, subln_g, layer)
    kc, vc = heads(kc, WIN_KV_HEADS), heads(vc, WIN_KV_HEADS)
    o_c = ungroup(dense_attention(group_q(heads(qc, WIN_Q_HEADS), WIN_KV_HEADS), kc, vc, sink))
    kd, vd = rms_norm(heads(kd, GQA_KV_HEADS), kn_g), heads(vd, GQA_KV_HEADS)
    o_d = ungroup(dense_attention(group_q(rms_norm(heads(qd, GQA_Q_HEADS), qn_g), GQA_KV_HEADS), kd, vd))
    out = merge_branches((o_a, o_b, o_c, o_d), gates, w_branch, w_out)
    kv = (jnp.stack([ka, va], axis=1), jnp.stack([kb, vb], axis=1),
          jnp.stack([kc, vc], axis=1), jnp.stack([kd, vd], axis=1))
    return out, kv


def latent_mixer(h, kv_na, kv_diff, kv_win, kv_gqa, w_in, rpb, lam_p, subln_g, sink, qn_g, kn_g,
                 w_branch, w_out, layer):
    qa, ka, va, qb, kb, vb, qc, kc, vc, qd, kd, vd, gates = project(h, w_in)
    o_a = neighbourhood_attention(heads(qa, NA_HEADS), heads(ka, NA_HEADS), heads(va, NA_HEADS),
                                  kv_na[:, 0], kv_na[:, 1], rpb)
    q12 = axial_rope(split_halves(heads(qb, DIFF_HEADS)))
    k12 = jnp.concatenate([axial_rope(split_halves(heads(kb, DIFF_HEADS))), split_halves(kv_diff[:, 0])], axis=3)
    v_b = jnp.concatenate([heads(vb, DIFF_HEADS), kv_diff[:, 1]], axis=2)
    o_b = diff_branch(q12, k12, v_b, lam_p, subln_g, layer)
    o_c = ungroup(window_attention(group_q(axial_rope(heads(qc, WIN_Q_HEADS)), WIN_KV_HEADS),
                                   axial_rope(heads(kc, WIN_KV_HEADS)), heads(vc, WIN_KV_HEADS),
                                   kv_win[:, 0], kv_win[:, 1], sink))
    q_d = group_q(axial_rope(rms_norm(heads(qd, GQA_Q_HEADS), qn_g)), GQA_KV_HEADS)
    k_d = jnp.concatenate([axial_rope(rms_norm(heads(kd, GQA_KV_HEADS), kn_g)), kv_gqa[:, 0]], axis=2)
    v_d = jnp.concatenate([heads(vd, GQA_KV_HEADS), kv_gqa[:, 1]], axis=2)
    o_d = ungroup(dense_attention(q_d, k_d, v_d))
    return merge_branches((o_a, o_b, o_c, o_d), gates, w_branch, w_out)


def expert_choice_ffn(h, w_router, w_gate, w_up, w_down):
    B, N, _ = h.shape
    cap = CAPACITY_FACTOR * N // N_EXPERTS
    aff = jax.nn.softmax(jnp.einsum('bnd,de->bne', h, w_router).astype(jnp.float32), axis=-1)
    vals, idx = lax.top_k(jnp.swapaxes(aff, 1, 2), cap)
    bidx = jnp.arange(B)[:, None, None]
    xs = h[bidx, idx]
    a = jnp.einsum('becd,edf->becf', xs, w_gate)
    u = jnp.einsum('becd,edf->becf', xs, w_up)
    y = jnp.einsum('becf,efd->becd', jax.nn.silu(a) * u, w_down) * vals[..., None].astype(h.dtype)
    return jnp.zeros_like(h).at[bidx, idx].add(y)


def adaln(cond, w_mod, b_mod):
    m = jnp.einsum('bd,de->be', jax.nn.silu(cond), w_mod) + b_mod
    return [t[:, None, :] for t in jnp.split(m, 6, axis=-1)]


def setup_inputs(seed: int = 0) -> dict:
    key = jax.random.key(seed)
    ks = jax.random.split(key, 32)

    def nrm(k, shape, s):
        return jax.random.normal(k, shape, jnp.float32) * s

    return {
        "x_prompt": nrm(ks[0], (BATCH, SEQ, D_MODEL), 1.0),
        "x_sample": nrm(ks[1], (DEC_BATCH, DEC_SEQ, D_MODEL), 1.0),
        "cache_kv_na": nrm(ks[2], (DEC_BATCH, DEPTH, 2, NA_HEADS, PAST_LEN, HEAD_DIM), 1.0),
        "cache_kv_diff": nrm(ks[3], (DEC_BATCH, DEPTH, 2, DIFF_HEADS, PAST_LEN, HEAD_DIM), 1.0),
        "cache_kv_win": nrm(ks[4], (DEC_BATCH, DEPTH, 2, WIN_KV_HEADS, PAST_LEN, HEAD_DIM), 1.0),
        "cache_kv_gqa": nrm(ks[5], (DEC_BATCH, DEPTH, 2, GQA_KV_HEADS, PAST_LEN, HEAD_DIM), 1.0),
        "c": nrm(ks[6], (DEC_BATCH, D_MODEL), 1.0),
        "c_ctx": nrm(ks[7], (D_MODEL,), 1.0),
        "w_mod": nrm(ks[8], (DEPTH, D_MODEL, 6 * D_MODEL), 0.5 * D_MODEL ** -0.5),
        "b_mod": nrm(ks[9], (DEPTH, 6 * D_MODEL), 0.02),
        "norm_mix_g": 1.0 + nrm(ks[10], (DEPTH, D_MODEL), 0.02),
        "norm_ffn_g": 1.0 + nrm(ks[11], (DEPTH, D_MODEL), 0.02),
        "w_in": nrm(ks[12], (DEPTH, D_MODEL, PROJ_WIDTH), D_MODEL ** -0.5),
        "na_rel_bias": nrm(ks[13], (DEPTH, NA_HEADS, 2 * NA_WIN_ROWS - 1, 2 * NA_WIN_COLS - 1), 0.5),
        "diff_lambda": nrm(ks[14], (DEPTH, 4, DIFF_HALF), 0.1),
        "diff_subln_g": 1.0 + nrm(ks[15], (DEPTH, HEAD_DIM), 0.02),
        "win_sink": nrm(ks[16], (DEPTH, WIN_KV_HEADS, WIN_Q_HEADS // WIN_KV_HEADS), 0.5),
        "gqa_q_norm_g": 1.0 + nrm(ks[17], (DEPTH, HEAD_DIM), 0.02),
        "gqa_k_norm_g": 1.0 + nrm(ks[18], (DEPTH, HEAD_DIM), 0.02),
        "w_branch": nrm(ks[19], (DEPTH, N_BRANCH, BRANCH_WIDTH, D_MODEL), BRANCH_WIDTH ** -0.5),
        "w_out": nrm(ks[20], (DEPTH, D_MODEL, D_MODEL), D_MODEL ** -0.5),
        "w_router": nrm(ks[21], (DEPTH, D_MODEL, N_EXPERTS), D_MODEL ** -0.5),
        "w_gate": nrm(ks[22], (DEPTH, N_EXPERTS, D_MODEL, EXPERT_FF), D_MODEL ** -0.5),
        "w_up": nrm(ks[23], (DEPTH, N_EXPERTS, D_MODEL, EXPERT_FF), D_MODEL ** -0.5),
        "w_down": nrm(ks[24], (DEPTH, N_EXPERTS, EXPERT_FF, D_MODEL), EXPERT_FF ** -0.5),
        "final_norm_g": 1.0 + nrm(ks[25], (D_MODEL,), 0.02),
    }


def reference(x_prompt, x_sample, cache_kv_na, cache_kv_diff, cache_kv_win, cache_kv_gqa, c, c_ctx,
              w_mod, b_mod, norm_mix_g, norm_ffn_g, w_in, na_rel_bias, diff_lambda, diff_subln_g,
              win_sink, gqa_q_norm_g, gqa_k_norm_g, w_branch, w_out, w_router, w_gate, w_up, w_down,
              final_norm_g):
    xp = x_prompt
    kv_na_l, kv_diff_l, kv_win_l, kv_gqa_l = [], [], [], []
    for l in range(DEPTH):
        sh1, sc1, g1, sh2, sc2, g2 = adaln(c_ctx[None, :], w_mod[l], b_mod[l])
        h = rms_norm(xp, norm_mix_g[l]) * (1 + sc1) + sh1
        mix, (kv_a, kv_b, kv_c, kv_d) = context_mixer(h, w_in[l], diff_lambda[l], diff_subln_g[l], win_sink[l],
                                                      gqa_q_norm_g[l], gqa_k_norm_g[l], w_branch[l], w_out[l], l)
        kv_na_l.append(kv_a)
        kv_diff_l.append(kv_b)
        kv_win_l.append(kv_c)
        kv_gqa_l.append(kv_d)
        xp = xp + g1 * mix
        h = rms_norm(xp, norm_ffn_g[l]) * (1 + sc2) + sh2
        xp = xp + g2 * expert_choice_ffn(h, w_router[l], w_gate[l], w_up[l], w_down[l])
    y_prompt = rms_norm(xp, final_norm_g)
    kv_na = jnp.stack(kv_na_l, axis=1)
    kv_diff = jnp.stack(kv_diff_l, axis=1)
    kv_win = jnp.stack(kv_win_l, axis=1)
    kv_gqa = jnp.stack(kv_gqa_l, axis=1)

    xs = x_sample
    for l in range(DEPTH):
        sh1, sc1, g1, sh2, sc2, g2 = adaln(c, w_mod[l], b_mod[l])
        h = rms_norm(xs, norm_mix_g[l]) * (1 + sc1) + sh1
        mix = latent_mixer(h, cache_kv_na[:, l], cache_kv_diff[:, l], cache_kv_win[:, l], cache_kv_gqa[:, l],
                           w_in[l], na_rel_bias[l], diff_lambda[l], diff_subln_g[l], win_sink[l],
                           gqa_q_norm_g[l], gqa_k_norm_g[l], w_branch[l], w_out[l], l)
        xs = xs + g1 * mix
        h = rms_norm(xs, norm_ffn_g[l]) * (1 + sc2) + sh2
        xs = xs + g2 * expert_choice_ffn(h, w_router[l], w_gate[l], w_up[l], w_down[l])
    y_sample = rms_norm(xs, final_norm_g)
    return (y_prompt, y_sample, kv_na, kv_diff, kv_win, kv_gqa)
```

```python
import functools
import math

import numpy as np
import jax
import jax.numpy as jnp
from jax import lax
from jax.experimental import pallas as pl
from jax.experimental.pallas import tpu as pltpu

F32 = jnp.float32
BF16 = jnp.bfloat16

D_MODEL = 2048
BATCH = 16
SEQ = 256
DEPTH = 2
DEC_BATCH = 2
DEC_SEQ = 2048
PAST_LEN = 256
GRID_W = 64
GRID_ROWS = DEC_SEQ // GRID_W
HEAD_DIM = 128
N_BRANCH = 4
BRANCH_WIDTH = 512
NA_WIN_ROWS = 8
NA_WIN_COLS = 16
WINDOW = 128
ROPE_BASE = 10000.0
N_EXPERTS = 16
EXPERT_FF = D_MODEL
CAPACITY_FACTOR = 2
EPS = 1e-6
QKV_WIDTH = 5120
PROJ_WIDTH = QKV_WIDTH + N_BRANCH * D_MODEL
T_CTX = BATCH * SEQ
T_LAT = DEC_BATCH * DEC_SEQ
MOD_ROWS = 8
NEG = -1e30

VMEM_LIMIT_V7X = 56 * 1024 * 1024
LANES = 128

QA, KA, VA = 0, 4, 8
QB, KB, VB = 12, 16, 20
QC, KC, VC = 24, 28, 30
QD, KD, VD = 32, 36, 38


def _params(*sem):
    return pltpu.CompilerParams(dimension_semantics=sem, vmem_limit_bytes=VMEM_LIMIT_V7X)


def _sigmoid(x):
    return 1.0 / (1.0 + jnp.exp(-x))


def _dot(a, b):
    return jnp.dot(a, b, preferred_element_type=F32)


def _dot_nt(a, b):
    return lax.dot_general(a, b, (((1,), (1,)), ((), ())), preferred_element_type=F32)


def _rms(x, g):
    return x * lax.rsqrt(jnp.mean(x * x, axis=-1, keepdims=True) + EPS) * g


def _mod_kernel(c_ref, w_ref, b_ref, o_ref):
    c = c_ref[...]
    s = c * _sigmoid(c)
    o_ref[...] = _dot(s.astype(BF16), w_ref[...].astype(BF16)) + b_ref[...]


def _modulation(cond, w_mod, b_mod):
    tn = 1024
    n = 6 * D_MODEL
    return pl.pallas_call(
        _mod_kernel,
        grid=(DEPTH, n // tn),
        in_specs=[pl.BlockSpec((MOD_ROWS, D_MODEL), lambda l, j: (0, 0)),
                  pl.BlockSpec((None, D_MODEL, tn), lambda l, j: (l, 0, j)),
                  pl.BlockSpec((None, 1, tn), lambda l, j: (l, 0, j))],
        out_specs=pl.BlockSpec((None, MOD_ROWS, tn), lambda l, j: (l, 0, j)),
        out_shape=jax.ShapeDtypeStruct((DEPTH, MOD_ROWS, n), F32),
        compiler_params=_params("arbitrary", "arbitrary"),
        name="adaln_mod",
    )(cond, w_mod, b_mod.reshape(DEPTH, 1, n))


def _normmod_kernel(*refs, modulate, router, out_dtype):
    x_ref, g_ref = refs[0], refs[1]
    i = 2
    h = _rms(x_ref[...], g_ref[...])
    if modulate:
        sc_ref, sh_ref = refs[i], refs[i + 1]
        i += 2
        h = h * (1.0 + sc_ref[...]) + sh_ref[...]
    if router:
        wr_ref = refs[i]
        i += 1
    h_ref = refs[i]
    h_ref[...] = h.astype(out_dtype)
    if router:
        aff_ref = refs[i + 1]
        logits = jnp.dot(h, wr_ref[...], precision=lax.Precision.HIGHEST, preferred_element_type=F32)
        lane = lax.broadcasted_iota(jnp.int32, logits.shape, 1)
        logits = jnp.where(lane < N_EXPERTS, logits, NEG)
        m = jnp.max(logits, axis=-1, keepdims=True)
        e = jnp.exp(logits - m)
        aff_ref[...] = e / jnp.sum(e, axis=-1, keepdims=True)


def _mod_row_map(tm, rows_per_req, mod_base, chunk):
    if rows_per_req is None:
        return lambda m: (mod_base, 0, chunk)
    return lambda m: (mod_base + (m * tm) // rows_per_req, 0, chunk)


def _normmod(x, g, mod3=None, *, sc_chunk=None, sh_chunk=None, rows_per_req=None, mod_base=0,
             w_router=None, out_dtype=BF16, name="normmod"):
    t = x.shape[0]
    tm = 256
    modulate = mod3 is not None
    router = w_router is not None
    in_specs = [pl.BlockSpec((tm, D_MODEL), lambda m: (m, 0)),
                pl.BlockSpec((1, D_MODEL), lambda m: (0, 0))]
    args = [x, g.reshape(1, D_MODEL)]
    if modulate:
        in_specs += [pl.BlockSpec((None, 1, D_MODEL), _mod_row_map(tm, rows_per_req, mod_base, sc_chunk)),
                     pl.BlockSpec((None, 1, D_MODEL), _mod_row_map(tm, rows_per_req, mod_base, sh_chunk))]
        args += [mod3, mod3]
    out_specs = [pl.BlockSpec((tm, D_MODEL), lambda m: (m, 0))]
    out_shape = [jax.ShapeDtypeStruct((t, D_MODEL), out_dtype)]
    if router:
        in_specs.append(pl.BlockSpec((D_MODEL, LANES), lambda m: (0, 0)))
        args.append(w_router)
        out_specs.append(pl.BlockSpec((tm, LANES), lambda m: (m, 0)))
        out_shape.append(jax.ShapeDtypeStruct((t, LANES), F32))
    res = pl.pallas_call(
        functools.partial(_normmod_kernel, modulate=modulate, router=router, out_dtype=out_dtype),
        grid=(t // tm,),
        in_specs=in_specs, out_specs=out_specs, out_shape=out_shape,
        compiler_params=_params("arbitrary"),
        name=name,
    )(*args)
    return res if router else res[0]


def _inproj_kernel(hc_ref, hl_ref, w_ref, pc_ref, pl_ref, wbf_ref):
    @pl.when(pl.program_id(1) == 0)
    def _():
        wbf_ref[...] = w_ref[...].astype(BF16)

    w = wbf_ref[...]
    pc_ref[...] = _dot(hc_ref[...], w)
    pl_ref[...] = _dot(hl_ref[...], w)


def _inproj(hc, hl, w_in, layer):
    tm, tn = 512, 1024
    return pl.pallas_call(
        _inproj_kernel,
        grid=(PROJ_WIDTH // tn, T_CTX // tm),
        in_specs=[pl.BlockSpec((tm, D_MODEL), lambda n, m: (m, 0)),
                  pl.BlockSpec((tm, D_MODEL), lambda n, m: (m, 0)),
                  pl.BlockSpec((None, D_MODEL, tn), lambda n, m: (layer, 0, n))],
        out_specs=[pl.BlockSpec((tm, tn), lambda n, m: (m, n)),
                   pl.BlockSpec((tm, tn), lambda n, m: (m, n))],
        out_shape=[jax.ShapeDtypeStruct((T_CTX, PROJ_WIDTH), F32),
                   jax.ShapeDtypeStruct((T_LAT, PROJ_WIDTH), F32)],
        scratch_shapes=[pltpu.VMEM((D_MODEL, tn), BF16)],
        compiler_params=_params("arbitrary", "arbitrary"),
        name="in_proj",
    )(hc, hl, w_in)


def _softmax_pv(s, v_bf, sink=None):
    m = jnp.max(s, axis=-1, keepdims=True)
    if sink is not None:
        m = jnp.maximum(m, sink)
    e = jnp.exp(s - m)
    den = jnp.sum(e, axis=-1, keepdims=True)
    if sink is not None:
        den = den + jnp.exp(sink - m)
    return _dot(e.astype(BF16), v_bf) / den


def _softmax(s):
    m = jnp.max(s, axis=-1, keepdims=True)
    e = jnp.exp(s - m)
    return e / jnp.sum(e, axis=-1, keepdims=True)


def _diff_lambda(lam_ref, lam_init):
    lp = lam_ref[...]
    s1 = jnp.sum(lp[0:1] * lp[1:2], axis=-1, keepdims=True)
    s2 = jnp.sum(lp[2:3] * lp[3:4], axis=-1, keepdims=True)
    return jnp.exp(s1) - jnp.exp(s2) + lam_init


def _half_masks(shape):
    lane = lax.broadcasted_iota(jnp.int32, shape, 1)
    return lane < (HEAD_DIM // 2)


def _rope(x, cos, sin_fwd, sin_bwd, quarter):
    fwd = pltpu.roll(x, LANES - quarter, axis=1)
    bwd = pltpu.roll(x, quarter, axis=1)
    return x * cos + fwd * sin_fwd + bwd * sin_bwd


def _rope_tables(dim):
    nf = dim // 4
    t = jnp.arange(DEC_SEQ)
    row = (t // GRID_W).astype(F32)
    col = (t % GRID_W).astype(F32)
    inv = ROPE_BASE ** (-jnp.arange(nf, dtype=F32) / nf)
    ar, ac = row[:, None] * inv, col[:, None] * inv
    ang = jnp.concatenate([ar, ar, ac, ac], axis=-1)
    ang = jnp.tile(ang, (1, HEAD_DIM // dim))
    cos, sin = jnp.cos(ang), jnp.sin(ang)
    first = ((np.arange(HEAD_DIM) // nf) % 2) == 0
    sin_fwd = jnp.where(first[None, :], -sin, 0.0)
    sin_bwd = jnp.where(first[None, :], 0.0, sin)
    return cos, sin_fwd, sin_bwd


def _ctx_attn_kernel(p_ref, lam_ref, subg_ref, sink_ref, qn_ref, kn_ref,
                     oa_ref, ob_ref, oc_ref, od_ref, kva_ref, kvb_ref, kvc_ref, kvd_ref, *, lam_init):
    def col(blk):
        return p_ref[:, blk * LANES:(blk + 1) * LANES]

    scale = HEAD_DIM ** -0.5
    for h in range(4):
        q, k, v = col(QA + h), col(KA + h), col(VA + h)
        kva_ref[0, h] = k
        kva_ref[1, h] = v
        s = _dot_nt(q.astype(BF16), k.astype(BF16)) * scale
        oa_ref[:, h * LANES:(h + 1) * LANES] = _softmax_pv(s, v.astype(BF16)).astype(BF16)
    lam = _diff_lambda(lam_ref, lam_init)
    first = _half_masks((SEQ, HEAD_DIM))
    for h in range(4):
        q, k, v = col(QB + h), col(KB + h), col(VB + h)
        kvb_ref[0, h] = k
        kvb_ref[1, h] = v
        kb = k.astype(BF16)
        q1 = jnp.where(first, q, 0.0).astype(BF16)
        q2 = jnp.where(first, 0.0, q).astype(BF16)
        dscale = (HEAD_DIM // 2) ** -0.5
        p1 = _softmax(_dot_nt(q1, kb) * dscale)
        p2 = _softmax(_dot_nt(q2, kb) * dscale)
        o = _dot((p1 - lam * p2).astype(BF16), v.astype(BF16))
        o = _rms(o, subg_ref[...]) * (1.0 - lam_init)
        ob_ref[:, h * LANES:(h + 1) * LANES] = o.astype(BF16)
    for kvh in range(2):
        k, v = col(KC + kvh), col(VC + kvh)
        kvc_ref[0, kvh] = k
        kvc_ref[1, kvh] = v
        kb, vb = k.astype(BF16), v.astype(BF16)
        for g in range(2):
            qh = kvh * 2 + g
            s = _dot_nt(col(QC + qh).astype(BF16), kb) * scale
            oc_ref[:, qh * LANES:(qh + 1) * LANES] = _softmax_pv(s, vb, sink=sink_ref[qh]).astype(BF16)
    for kvh in range(2):
        k, v = _rms(col(KD + kvh), kn_ref[...]), col(VD + kvh)
        kvd_ref[0, kvh] = k
        kvd_ref[1, kvh] = v
        kb, vb = k.astype(BF16), v.astype(BF16)
        for g in range(2):
            qh = kvh * 2 + g
            q = _rms(col(QD + qh), qn_ref[...])
            s = _dot_nt(q.astype(BF16), kb) * scale
            od_ref[:, qh * LANES:(qh + 1) * LANES] = _softmax_pv(s, vb).astype(BF16)


def _ctx_attention(p_ctx, lam_p, subln_g, sink, qn_g, kn_g, layer):
    lam_init = 0.8 - 0.6 * math.exp(-0.3 * layer)
    vec = lambda: pl.BlockSpec((1, HEAD_DIM), lambda b: (0, 0))
    o_spec = pl.BlockSpec((SEQ, BRANCH_WIDTH), lambda b: (b, 0))
    o_shape = jax.ShapeDtypeStruct((T_CTX, BRANCH_WIDTH), BF16)

    def kv_spec(nh):
        return pl.BlockSpec((None, 2, nh, SEQ, HEAD_DIM), lambda b: (b, 0, 0, 0, 0))

    def kv_shape(nh):
        return jax.ShapeDtypeStruct((BATCH, 2, nh, SEQ, HEAD_DIM), F32)

    return pl.pallas_call(
        functools.partial(_ctx_attn_kernel, lam_init=lam_init),
        grid=(BATCH,),
        in_specs=[pl.BlockSpec((SEQ, QKV_WIDTH), lambda b: (b, 0)),
                  pl.BlockSpec((4, HEAD_DIM // 2), lambda b: (0, 0)),
                  vec(),
                  pl.BlockSpec(memory_space=pltpu.SMEM),
                  vec(), vec()],
        out_specs=[o_spec, o_spec, o_spec, o_spec, kv_spec(4), kv_spec(4), kv_spec(2), kv_spec(2)],
        out_shape=[o_shape, o_shape, o_shape, o_shape, kv_shape(4), kv_shape(4), kv_shape(2), kv_shape(2)],
        compiler_params=_params("arbitrary"),
        name="ctx_attention",
    )(p_ctx, lam_p, subln_g.reshape(1, HEAD_DIM), sink.reshape(4), qn_g.reshape(1, HEAD_DIM),
      kn_g.reshape(1, HEAD_DIM))


def _na_bias_slabs(rpb):
    c = np.arange(GRID_W)
    cs = np.clip(c - NA_WIN_COLS // 2, 0, GRID_W - NA_WIN_COLS)
    valid = (c[None, :] >= cs[:, None]) & (c[None, :] < cs[:, None] + NA_WIN_COLS)
    dc = np.clip(c[None, :] - c[:, None], -(NA_WIN_COLS - 1), NA_WIN_COLS - 1) + NA_WIN_COLS - 1
    dr = np.arange(NA_WIN_ROWS)[:, None] + np.arange(NA_WIN_ROWS)[None, :]
    b = rpb.astype(F32)[:, dr[:, :, None, None], dc[None, None, :, :]]
    b = jnp.where(valid[None, None, None], b, NEG)
    return b.transpose(0, 1, 3, 2, 4).reshape(rpb.shape[0], NA_WIN_ROWS, GRID_W, NA_WIN_ROWS * GRID_W)


def _lat_na_kernel(q_ref, k_ref, v_ref, kc_ref, vc_ref, bias_ref, o_ref, q_s, k_s, v_s, sctx_s):
    scale = HEAD_DIM ** -0.5
    q_s[...] = q_ref[...].astype(BF16)
    k_s[...] = k_ref[...].astype(BF16)
    v_s[...] = v_ref[...].astype(BF16)
    kc = kc_ref[...].astype(BF16)
    vc = vc_ref[...].astype(BF16)
    sctx_s[...] = _dot_nt(q_s[...], kc) * scale
    nwin = NA_WIN_ROWS * GRID_W

    def body(r, carry):
        rs = jnp.clip(r - NA_WIN_ROWS // 2, 0, GRID_ROWS - NA_WIN_ROWS)
        dr0 = rs - r + NA_WIN_ROWS - 1
        q0 = pl.multiple_of(r * GRID_W, GRID_W)
        k0 = pl.multiple_of(rs * GRID_W, GRID_W)
        q = q_s[pl.ds(q0, GRID_W), :]
        s_nb = _dot_nt(q, k_s[pl.ds(k0, nwin), :]) * scale + bias_ref[dr0]
        s_c = sctx_s[pl.ds(q0, GRID_W), :]
        m = jnp.maximum(jnp.max(s_nb, axis=-1, keepdims=True), jnp.max(s_c, axis=-1, keepdims=True))
        e_nb = jnp.exp(s_nb - m)
        e_c = jnp.exp(s_c - m)
        den = jnp.sum(e_nb, axis=-1, keepdims=True) + jnp.sum(e_c, axis=-1, keepdims=True)
        o = _dot(e_nb.astype(BF16), v_s[pl.ds(k0, nwin), :]) + _dot(e_c.astype(BF16), vc)
        o_ref[pl.ds(q0, GRID_W), :] = (o / den).astype(BF16)
        return carry

    lax.fori_loop(0, GRID_ROWS, body, 0)


def _cache_spec(layer, which, head_fn):
    return pl.BlockSpec((None, None, None, None, PAST_LEN, HEAD_DIM),
                        lambda b, h: (b, layer, which, head_fn(h), 0, 0))


def _lat_na(p_lat, cache, bias_slabs, layer):
    blk = lambda off: pl.BlockSpec((DEC_SEQ, HEAD_DIM), lambda b, h: (b, off + h))
    return pl.pallas_call(
        _lat_na_kernel,
        grid=(DEC_BATCH, 4),
        in_specs=[blk(QA), blk(KA), blk(VA),
                  _cache_spec(layer, 0, lambda h: h), _cache_spec(layer, 1, lambda h: h),
                  pl.BlockSpec((None, NA_WIN_ROWS, GRID_W, NA_WIN_ROWS * GRID_W), lambda b, h: (h, 0, 0, 0))],
        out_specs=pl.BlockSpec((DEC_SEQ, HEAD_DIM), lambda b, h: (b, h)),
        out_shape=jax.ShapeDtypeStruct((T_LAT, BRANCH_WIDTH), BF16),
        scratch_shapes=[pltpu.VMEM((DEC_SEQ, HEAD_DIM), BF16), pltpu.VMEM((DEC_SEQ, HEAD_DIM), BF16),
                        pltpu.VMEM((DEC_SEQ, HEAD_DIM), BF16), pltpu.VMEM((DEC_SEQ, PAST_LEN), F32)],
        compiler_params=_params("arbitrary", "arbitrary"),
        name="lat_na",
    )(p_lat, p_lat, p_lat, cache, cache, bias_slabs)


DENSE_TQ = 256
N_KEYS = DEC_SEQ + PAST_LEN


def _lat_diff_kernel(q_ref, k_ref, v_ref, kc_ref, vc_ref, cos_ref, sf_ref, sb_ref, lam_ref, subg_ref,
                     o_ref, q_s, k_s, v_s, *, lam_init):
    cos, sf, sb = cos_ref[...], sf_ref[...], sb_ref[...]
    quarter = HEAD_DIM // 8
    q_s[...] = _rope(q_ref[...], cos, sf, sb, quarter)
    k_s[0:DEC_SEQ, :] = _rope(k_ref[...], cos, sf, sb, quarter).astype(BF16)
    k_s[DEC_SEQ:N_KEYS, :] = kc_ref[...].astype(BF16)
    v_s[0:DEC_SEQ, :] = v_ref[...].astype(BF16)
    v_s[DEC_SEQ:N_KEYS, :] = vc_ref[...].astype(BF16)
    lam = _diff_lambda(lam_ref, lam_init)
    first = _half_masks((DENSE_TQ, HEAD_DIM))
    dscale = (HEAD_DIM // 2) ** -0.5
    subg = subg_ref[...]

    def body(i, carry):
        r0 = pl.multiple_of(i * DENSE_TQ, DENSE_TQ)
        q = q_s[pl.ds(r0, DENSE_TQ), :]
        qq = jnp.concatenate([jnp.where(first, q, 0.0), jnp.where(first, 0.0, q)], axis=0).astype(BF16)
        p = _softmax(_dot_nt(qq, k_s[...]) * dscale)
        a = (p[0:DENSE_TQ] - lam * p[DENSE_TQ:2 * DENSE_TQ]).astype(BF16)
        o = _rms(_dot(a, v_s[...]), subg) * (1.0 - lam_init)
        o_ref[pl.ds(r0, DENSE_TQ), :] = o.astype(BF16)
        return carry

    lax.fori_loop(0, DEC_SEQ // DENSE_TQ, body, 0)


def _table_spec():
    return pl.BlockSpec((DEC_SEQ, HEAD_DIM), lambda b, h: (0, 0))


def _lat_diff(p_lat, cache, tables, lam_p, subln_g, layer):
    lam_init = 0.8 - 0.6 * math.exp(-0.3 * layer)
    blk = lambda off: pl.BlockSpec((DEC_SEQ, HEAD_DIM), lambda b, h: (b, off + h))
    return pl.pallas_call(
        functools.partial(_lat_diff_kernel, lam_init=lam_init),
        grid=(DEC_BATCH, 4),
        in_specs=[blk(QB), blk(KB), blk(VB),
                  _cache_spec(layer, 0, lambda h: h), _cache_spec(layer, 1, lambda h: h),
                  _table_spec(), _table_spec(), _table_spec(),
                  pl.BlockSpec((4, HEAD_DIM // 2), lambda b, h: (0, 0)),
                  pl.BlockSpec((1, HEAD_DIM), lambda b, h: (0, 0))],
        out_specs=pl.BlockSpec((DEC_SEQ, HEAD_DIM), lambda b, h: (b, h)),
        out_shape=jax.ShapeDtypeStruct((T_LAT, BRANCH_WIDTH), BF16),
        scratch_shapes=[pltpu.VMEM((DEC_SEQ, HEAD_DIM), F32), pltpu.VMEM((N_KEYS, HEAD_DIM), BF16),
                        pltpu.VMEM((N_KEYS, HEAD_DIM), BF16)],
        compiler_params=_params("arbitrary", "arbitrary"),
        name="lat_diff",
    )(p_lat, p_lat, p_lat, cache, cache, *tables, lam_p, subln_g.reshape(1, HEAD_DIM))


WIN_KEYS = 3 * WINDOW


def _lat_win_kernel(q_ref, k_ref, v_ref, kc_ref, vc_ref, cos_ref, sf_ref, sb_ref, sink_ref, o_ref,
                    q_s, k_s, v_s):
    g = pl.program_id(1)
    cos, sf, sb = cos_ref[...], sf_ref[...], sb_ref[...]
    quarter = HEAD_DIM // 4
    scale = HEAD_DIM ** -0.5
    for j in range(2):
        q_s[j] = _rope(q_ref[:, j * LANES:(j + 1) * LANES], cos, sf, sb, quarter).astype(BF16)
    k_s[...] = _rope(k_ref[...], cos, sf, sb, quarter).astype(BF16)
    v_s[...] = v_ref[...].astype(BF16)
    kc = kc_ref[...].astype(BF16)
    vc = vc_ref[...].astype(BF16)
    rows = lax.broadcasted_iota(jnp.int32, (2 * WINDOW, 1), 0)
    sink = jnp.where(rows < WINDOW, sink_ref[2 * g], sink_ref[2 * g + 1])
    qi = lax.broadcasted_iota(jnp.int32, (2 * WINDOW, WIN_KEYS), 0) & (WINDOW - 1)
    kj = lax.broadcasted_iota(jnp.int32, (2 * WINDOW, WIN_KEYS), 1)
    rel = qi - kj

    def body(n, carry):
        q0 = pl.multiple_of(n * WINDOW, WINDOW)
        k0 = pl.multiple_of(jnp.clip((n - 1) * WINDOW, 0, DEC_SEQ - WIN_KEYS), WINDOW)
        q = jnp.concatenate([q_s[0, pl.ds(q0, WINDOW), :], q_s[1, pl.ds(q0, WINDOW), :]], axis=0)
        s_loc = _dot_nt(q, k_s[pl.ds(k0, WIN_KEYS), :]) * scale
        dist = rel + (q0 - k0)
        s_loc = jnp.where(dist <= WINDOW, jnp.where(dist >= -WINDOW, s_loc, NEG), NEG)
        s_c = _dot_nt(q, kc) * scale
        m = jnp.maximum(jnp.maximum(jnp.max(s_loc, axis=-1, keepdims=True),
                                    jnp.max(s_c, axis=-1, keepdims=True)), sink)
        e_loc = jnp.exp(s_loc - m)
        e_c = jnp.exp(s_c - m)
        den = (jnp.sum(e_loc, axis=-1, keepdims=True) + jnp.sum(e_c, axis=-1, keepdims=True)
               + jnp.exp(sink - m))
        o = (_dot(e_loc.astype(BF16), v_s[pl.ds(k0, WIN_KEYS), :]) + _dot(e_c.astype(BF16), vc)) / den
        o_ref[pl.ds(q0, WINDOW), 0:LANES] = o[0:WINDOW].astype(BF16)
        o_ref[pl.ds(q0, WINDOW), LANES:2 * LANES] = o[WINDOW:2 * WINDOW].astype(BF16)
        return carry

    lax.fori_loop(0, DEC_SEQ // WINDOW, body, 0)


def _lat_win(p_lat, cache, tables, sink, layer):
    return pl.pallas_call(
        _lat_win_kernel,
        grid=(DEC_BATCH, 2),
        in_specs=[pl.BlockSpec((DEC_SEQ, 2 * HEAD_DIM), lambda b, g: (b, QC // 2 + g)),
                  pl.BlockSpec((DEC_SEQ, HEAD_DIM), lambda b, g: (b, KC + g)),
                  pl.BlockSpec((DEC_SEQ, HEAD_DIM), lambda b, g: (b, VC + g)),
                  _cache_spec(layer, 0, lambda g: g), _cache_spec(layer, 1, lambda g: g),
                  _table_spec(), _table_spec(), _table_spec(),
                  pl.BlockSpec(memory_space=pltpu.SMEM)],
        out_specs=pl.BlockSpec((DEC_SEQ, 2 * HEAD_DIM), lambda b, g: (b, g)),
        out_shape=jax.ShapeDtypeStruct((T_LAT, BRANCH_WIDTH), BF16),
        scratch_shapes=[pltpu.VMEM((2, DEC_SEQ, HEAD_DIM), BF16), pltpu.VMEM((DEC_SEQ, HEAD_DIM), BF16),
                        pltpu.VMEM((DEC_SEQ, HEAD_DIM), BF16)],
        compiler_params=_params("arbitrary", "arbitrary"),
        name="lat_win",
    )(p_lat, p_lat, p_lat, cache, cache, *tables, sink.reshape(4))


def _lat_gqa_kernel(q_ref, k_ref, v_ref, kc_ref, vc_ref, cos_ref, sf_ref, sb_ref, qn_ref, kn_ref, o_ref,
                    q_s, k_s, v_s):
    cos, sf, sb = cos_ref[...], sf_ref[...], sb_ref[...]
    quarter = HEAD_DIM // 4
    scale = HEAD_DIM ** -0.5
    for j in range(2):
        qn = _rms(q_ref[:, j * LANES:(j + 1) * LANES], qn_ref[...])
        q_s[j] = _rope(qn, cos, sf, sb, quarter).astype(BF16)
    k_s[0:DEC_SEQ, :] = _rope(_rms(k_ref[...], kn_ref[...]), cos, sf, sb, quarter).astype(BF16)
    k_s[DEC_SEQ:N_KEYS, :] = kc_ref[...].astype(BF16)
    v_s[0:DEC_SEQ, :] = v_ref[...].astype(BF16)
    v_s[DEC_SEQ:N_KEYS, :] = vc_ref[...].astype(BF16)

    def body(i, carry):
        r0 = pl.multiple_of(i * DENSE_TQ, DENSE_TQ)
        q = jnp.concatenate([q_s[0, pl.ds(r0, DENSE_TQ), :], q_s[1, pl.ds(r0, DENSE_TQ), :]], axis=0)
        o = _softmax_pv(_dot_nt(q, k_s[...]) * scale, v_s[...])
        o_ref[pl.ds(r0, DENSE_TQ), 0:LANES] = o[0:DENSE_TQ].astype(BF16)
        o_ref[pl.ds(r0, DENSE_TQ), LANES:2 * LANES] = o[DENSE_TQ:2 * DENSE_TQ].astype(BF16)
        return carry

    lax.fori_loop(0, DEC_SEQ // DENSE_TQ, body, 0)


def _lat_gqa(p_lat, cache, tables, qn_g, kn_g, layer):
    vec = lambda: pl.BlockSpec((1, HEAD_DIM), lambda b, g: (0, 0))
    return pl.pallas_call(
        _lat_gqa_kernel,
        grid=(DEC_BATCH, 2),
        in_specs=[pl.BlockSpec((DEC_SEQ, 2 * HEAD_DIM), lambda b, g: (b, QD // 2 + g)),
                  pl.BlockSpec((DEC_SEQ, HEAD_DIM), lambda b, g: (b, KD + g)),
                  pl.BlockSpec((DEC_SEQ, HEAD_DIM), lambda b, g: (b, VD + g)),
                  _cache_spec(layer, 0, lambda g: g), _cache_spec(layer, 1, lambda g: g),
                  _table_spec(), _table_spec(), _table_spec(), vec(), vec()],
        out_specs=pl.BlockSpec((DEC_SEQ, 2 * HEAD_DIM), lambda b, g: (b, g)),
        out_shape=jax.ShapeDtypeStruct((T_LAT, BRANCH_WIDTH), BF16),
        scratch_shapes=[pltpu.VMEM((2, DEC_SEQ, HEAD_DIM), BF16), pltpu.VMEM((N_KEYS, HEAD_DIM), BF16),
                        pltpu.VMEM((N_KEYS, HEAD_DIM), BF16)],
        compiler_params=_params("arbitrary", "arbitrary"),
        name="lat_gqa",
    )(p_lat, p_lat, p_lat, cache, cache, *tables, qn_g.reshape(1, HEAD_DIM), kn_g.reshape(1, HEAD_DIM))


def _branch_kernel(oa_ref, ob_ref, oc_ref, od_ref, ga_ref, gb_ref, gc_ref, gd_ref, w_ref, z_ref, wbf_ref):
    @pl.when(pl.program_id(1) == 0)
    def _():
        wbf_ref[...] = w_ref[...].astype(BF16)

    acc = None
    for i, (o_ref, g_ref) in enumerate(((oa_ref, ga_ref), (ob_ref, gb_ref), (oc_ref, gc_ref), (od_ref, gd_ref))):
        term = _sigmoid(g_ref[...]) * _dot(o_ref[...], wbf_ref[i])
        acc = term if acc is None else acc + term
    z_ref[...] = acc.astype(BF16)


def _branch_merge(outs, p, w_branch, layer):
    t = p.shape[0]
    tm, tn = 512, 512
    gate0 = QKV_WIDTH // tn
    o_spec = pl.BlockSpec((tm, BRANCH_WIDTH), lambda n, m: (m, 0))

    def gate_spec(i):
        return pl.BlockSpec((tm, tn), lambda n, m: (m, gate0 + i * (D_MODEL // tn) + n))

    return pl.pallas_call(
        _branch_kernel,
        grid=(D_MODEL // tn, t // tm),
        in_specs=[o_spec, o_spec, o_spec, o_spec, gate_spec(0), gate_spec(1), gate_spec(2), gate_spec(3),
                  pl.BlockSpec((None, N_BRANCH, BRANCH_WIDTH, tn), lambda n, m: (layer, 0, 0, n))],
        out_specs=pl.BlockSpec((tm, tn), lambda n, m: (m, n)),
        out_shape=jax.ShapeDtypeStruct((t, D_MODEL), BF16),
        scratch_shapes=[pltpu.VMEM((N_BRANCH, BRANCH_WIDTH, tn), BF16)],
        compiler_params=_params("arbitrary", "arbitrary"),
        name="branch_merge",
    )(*outs, p, p, p, p, w_branch)


def _outproj_kernel(zc_ref, zl_ref, xc_ref, xl_ref, gc_ref, gl_ref, w_ref, oc_ref, ol_ref, wbf_ref):
    @pl.when(pl.program_id(1) == 0)
    def _():
        wbf_ref[...] = w_ref[...].astype(BF16)

    w = wbf_ref[...]
    oc_ref[...] = xc_ref[...] + gc_ref[...] * _dot(zc_ref[...], w)
    ol_ref[...] = xl_ref[...] + gl_ref[...] * _dot(zl_ref[...], w)


def _outproj(zc, zl, xc, xl, mod3, w_out, layer, gate_chunk):
    tm, tn = 512, 512
    g0 = gate_chunk * (D_MODEL // tn)
    z_spec = pl.BlockSpec((tm, D_MODEL), lambda n, m: (m, 0))
    x_spec = pl.BlockSpec((tm, tn), lambda n, m: (m, n))
    shape = jax.ShapeDtypeStruct((T_CTX, D_MODEL), F32)
    return pl.pallas_call(
        _outproj_kernel,
        grid=(D_MODEL // tn, T_CTX // tm),
        in_specs=[z_spec, z_spec, x_spec, x_spec,
                  pl.BlockSpec((None, 1, tn), lambda n, m: (0, 0, g0 + n)),
                  pl.BlockSpec((None, 1, tn), lambda n, m: (1 + (m * tm) // DEC_SEQ, 0, g0 + n)),
                  pl.BlockSpec((None, D_MODEL, tn), lambda n, m: (layer, 0, n))],
        out_specs=[x_spec, x_spec],
        out_shape=[shape, shape],
        scratch_shapes=[pltpu.VMEM((D_MODEL, tn), BF16)],
        compiler_params=_params("arbitrary", "arbitrary"),
        name="out_proj",
    )(zc, zl, xc, xl, mod3, mod3, w_out)


def _select_kernel(aff_ref, tri_ref, pose_ref, post_ref, affe_ref, *, cap):
    a = aff_ref[...].T[0:N_EXPERTS, :]
    n = a.shape[1]
    bits = lax.bitcast_convert_type(a, jnp.int32)
    thr = jnp.zeros((N_EXPERTS, 1), jnp.int32)
    for bit in range(30, -1, -1):
        cand = thr | (1 << bit)
        cnt = jnp.sum(jnp.where(bits >= cand, 1.0, 0.0), axis=1, keepdims=True)
        thr = jnp.where(cnt >= cap, cand, thr)
    gt = bits > thr
    eq = bits == thr
    need = cap - jnp.sum(jnp.where(gt, 1.0, 0.0), axis=1, keepdims=True)
    tri = tri_ref[...]
    before_eq = _dot(jnp.where(eq, 1.0, 0.0).astype(BF16), tri)
    sel = jnp.where(gt, 1.0, jnp.where(eq, jnp.where(before_eq < need, 1.0, 0.0), 0.0))
    pos = _dot(sel.astype(BF16), tri)
    pos = jnp.where(sel > 0.5, pos, -1.0)
    pose_ref[...] = pos
    affe_ref[...] = a
    full = jnp.concatenate([pos, jnp.full((LANES - N_EXPERTS, n), -1.0, F32)], axis=0)
    post_ref[...] = full.T


def _select(aff, n_req, n_tok):
    cap = CAPACITY_FACTOR * n_tok // N_EXPERTS
    tri = (jnp.arange(n_tok)[:, None] < jnp.arange(n_tok)[None, :]).astype(BF16)
    e_spec = pl.BlockSpec((None, N_EXPERTS, n_tok), lambda b: (b, 0, 0))
    e_shape = jax.ShapeDtypeStruct((n_req, N_EXPERTS, n_tok), F32)
    return pl.pallas_call(
        functools.partial(_select_kernel, cap=cap),
        grid=(n_req,),
        in_specs=[pl.BlockSpec((n_tok, LANES), lambda b: (b, 0)),
                  pl.BlockSpec((n_tok, n_tok), lambda b: (0, 0))],
        out_specs=[e_spec, pl.BlockSpec((n_tok, LANES), lambda b: (b, 0)), e_spec],
        out_shape=[e_shape, jax.ShapeDtypeStruct((n_req * n_tok, LANES), F32), e_shape],
        compiler_params=_params("arbitrary"),
        name="expert_select",
    )(aff, tri)


def _gather_kernel(h_ref, pose_ref, affe_ref, xs_ref, val_ref, *, cap, eg):
    h = h_ref[...]
    n = h.shape[0]
    slot = lax.broadcasted_iota(jnp.int32, (cap, n), 0).astype(F32)
    for e in range(eg):
        hit = pose_ref[e] == slot
        xs_ref[e] = _dot(jnp.where(hit, 1.0, 0.0).astype(BF16), h).astype(BF16)
        val_ref[e] = jnp.sum(jnp.where(hit, affe_ref[e], 0.0), axis=1, keepdims=True)


def _gather(h, pose, affe, n_req, n_tok, eg):
    cap = CAPACITY_FACTOR * n_tok // N_EXPERTS
    r_spec = pl.BlockSpec((None, eg, 1, n_tok), lambda b, g: (b, g, 0, 0))
    return pl.pallas_call(
        functools.partial(_gather_kernel, cap=cap, eg=eg),
        grid=(n_req, N_EXPERTS // eg),
        in_specs=[pl.BlockSpec((n_tok, D_MODEL), lambda b, g: (b, 0)), r_spec, r_spec],
        out_specs=[pl.BlockSpec((eg, cap, D_MODEL), lambda b, g: (g, b, 0)),
                   pl.BlockSpec((eg, cap, 1), lambda b, g: (g, b, 0))],
        out_shape=[jax.ShapeDtypeStruct((N_EXPERTS, n_req * cap, D_MODEL), BF16),
                   jax.ShapeDtypeStruct((N_EXPERTS, n_req * cap, 1), F32)],
        compiler_params=_params("arbitrary", "arbitrary"),
        name="expert_gather",
    )(h, pose.reshape(n_req, N_EXPERTS, 1, n_tok), affe.reshape(n_req, N_EXPERTS, 1, n_tok))


def _expert_kernel(xc_ref, xl_ref, vc_ref, vl_ref, wg_ref, wu_ref, wd_ref, yc_ref, yl_ref, accc_ref, accl_ref):
    f = pl.program_id(1)
    wg = wg_ref[...].astype(BF16)
    wu = wu_ref[...].astype(BF16)
    wd = wd_ref[...].astype(BF16)
    for x_ref, acc_ref in ((xc_ref, accc_ref), (xl_ref, accl_ref)):
        x = x_ref[...]
        a = _dot(x, wg)
        u = _dot(x, wu)
        part = _dot((a * _sigmoid(a) * u).astype(BF16), wd)

        @pl.when(f == 0)
        def _():
            acc_ref[...] = part

        @pl.when(f > 0)
        def _():
            acc_ref[...] += part

    @pl.when(f == pl.num_programs(1) - 1)
    def _():
        yc_ref[...] = (accc_ref[...] * vc_ref[...]).astype(BF16)
        yl_ref[...] = (accl_ref[...] * vl_ref[...]).astype(BF16)


def _experts(xs_c, xs_l, val_c, val_l, w_gate, w_up, w_down, layer):
    tf = 256
    slots = xs_c.shape[1]
    x_spec = pl.BlockSpec((None, slots, D_MODEL), lambda e, f: (e, 0, 0))
    v_spec = pl.BlockSpec((None, slots, 1), lambda e, f: (e, 0, 0))
    shape = jax.ShapeDtypeStruct((N_EXPERTS, slots, D_MODEL), BF16)
    return pl.pallas_call(
        _expert_kernel,
        grid=(N_EXPERTS, EXPERT_FF // tf),
        in_specs=[x_spec, x_spec, v_spec, v_spec,
                  pl.BlockSpec((None, None, D_MODEL, tf), lambda e, f: (layer, e, 0, f)),
                  pl.BlockSpec((None, None, D_MODEL, tf), lambda e, f: (layer, e, 0, f)),
                  pl.BlockSpec((None, None, tf, D_MODEL), lambda e, f: (layer, e, f, 0))],
        out_specs=[x_spec, x_spec],
        out_shape=[shape, shape],
        scratch_shapes=[pltpu.VMEM((slots, D_MODEL), F32), pltpu.VMEM((slots, D_MODEL), F32)],
        compiler_params=_params("arbitrary", "arbitrary"),
        name="experts",
    )(xs_c, xs_l, val_c, val_l, w_gate, w_up, w_down)


COMBINE_LANES = 512


def _combine_kernel(x_ref, g_ref, post_ref, y_ref, o_ref, *, cap):
    eg = COMBINE_LANES // cap
    shift = int(math.log2(cap))
    post = post_ref[...].astype(BF16)
    lane = lax.broadcasted_iota(jnp.int32, (LANES, COMBINE_LANES), 1)
    row = lax.broadcasted_iota(jnp.int32, (LANES, COMBINE_LANES), 0)
    slot = (lax.broadcasted_iota(jnp.int32, (1, COMBINE_LANES), 1) & (cap - 1)).astype(F32)
    acc = None
    for grp in range(N_EXPERTS // eg):
        expand = jnp.where(row == grp * eg + (lane >> shift), 1.0, 0.0).astype(BF16)
        pe = _dot(post, expand)
        onehot = jnp.where(pe == slot, 1.0, 0.0).astype(BF16)
        yg = y_ref[grp * eg:(grp + 1) * eg].reshape(COMBINE_LANES, y_ref.shape[-1])
        term = _dot(onehot, yg)
        acc = term if acc is None else acc + term
    o_ref[...] = x_ref[...] + g_ref[...] * acc


def _combine(x, mod3, post, y, n_req, n_tok, gate_chunk, mod_base, per_req):
    cap = CAPACITY_FACTOR * n_tok // N_EXPERTS
    tn = 512
    g0 = gate_chunk * (D_MODEL // tn)
    if per_req:
        g_map = lambda b, j: (mod_base + b, 0, g0 + j)
    else:
        g_map = lambda b, j: (mod_base, 0, g0 + j)
    x_spec = pl.BlockSpec((n_tok, tn), lambda b, j: (b, j))
    return pl.pallas_call(
        functools.partial(_combine_kernel, cap=cap),
        grid=(n_req, D_MODEL // tn),
        in_specs=[x_spec,
                  pl.BlockSpec((None, 1, tn), g_map),
                  pl.BlockSpec((n_tok, LANES), lambda b, j: (b, 0)),
                  pl.BlockSpec((N_EXPERTS, cap, tn), lambda b, j: (0, b, j))],
        out_specs=x_spec,
        out_shape=jax.ShapeDtypeStruct(x.shape, F32),
        compiler_params=_params("arbitrary", "arbitrary"),
        name="expert_combine",
    )(x, mod3, post, y)


def kernel(x_prompt, x_sample, cache_kv_na, cache_kv_diff, cache_kv_win, cache_kv_gqa, c, c_ctx, w_mod, b_mod,
           norm_mix_g, norm_ffn_g, w_in, na_rel_bias, diff_lambda, diff_subln_g, win_sink, gqa_q_norm_g,
           gqa_k_norm_g, w_branch, w_out, w_router, w_gate, w_up, w_down, final_norm_g):
    cond = jnp.zeros((MOD_ROWS, D_MODEL), F32).at[0].set(c_ctx).at[1:1 + DEC_BATCH].set(c)
    mod = _modulation(cond, w_mod, b_mod)
    tab64 = _rope_tables(HEAD_DIM // 2)
    tab128 = _rope_tables(HEAD_DIM)
    xc = x_prompt.reshape(T_CTX, D_MODEL)
    xl = x_sample.reshape(T_LAT, D_MODEL)
    kvs = ([], [], [], [])
    for l in range(DEPTH):
        mod3 = mod[l].reshape(MOD_ROWS, 1, 6 * D_MODEL)
        lat = dict(rows_per_req=DEC_SEQ, mod_base=1)
        hc = _normmod(xc, norm_mix_g[l], mod3, sc_chunk=1, sh_chunk=0, name="norm_mix_ctx")
        hl = _normmod(xl, norm_mix_g[l], mod3, sc_chunk=1, sh_chunk=0, name="norm_mix_lat", **lat)
        pc, pl_ = _inproj(hc, hl, w_in, l)
        oa, ob, oc, od, kva, kvb, kvc, kvd = _ctx_attention(
            pc, diff_lambda[l], diff_subln_g[l], win_sink[l], gqa_q_norm_g[l], gqa_k_norm_g[l], l)
        for lst, kv in zip(kvs, (kva, kvb, kvc, kvd)):
            lst.append(kv)
        zc = _branch_merge((oa, ob, oc, od), pc, w_branch, l)
        la = _lat_na(pl_, cache_kv_na, _na_bias_slabs(na_rel_bias[l]), l)
        lb = _lat_diff(pl_, cache_kv_diff, tab64, diff_lambda[l], diff_subln_g[l], l)
        lc = _lat_win(pl_, cache_kv_win, tab128, win_sink[l], l)
        ld = _lat_gqa(pl_, cache_kv_gqa, tab128, gqa_q_norm_g[l], gqa_k_norm_g[l], l)
        zl = _branch_merge((la, lb, lc, ld), pl_, w_branch, l)
        xc, xl = _outproj(zc, zl, xc, xl, mod3, w_out, l, gate_chunk=2)
        wr = jnp.zeros((D_MODEL, LANES), F32).at[:, :N_EXPERTS].set(w_router[l])
        hc, affc = _normmod(xc, norm_ffn_g[l], mod3, sc_chunk=4, sh_chunk=3, w_router=wr, name="norm_ffn_ctx")
        hl, affl = _normmod(xl, norm_ffn_g[l], mod3, sc_chunk=4, sh_chunk=3, w_router=wr, name="norm_ffn_lat",
                            **lat)
        pose_c, post_c, affe_c = _select(affc, BATCH, SEQ)
        pose_l, post_l, affe_l = _select(affl, DEC_BATCH, DEC_SEQ)
        xs_c, val_c = _gather(hc, pose_c, affe_c, BATCH, SEQ, eg=N_EXPERTS)
        xs_l, val_l = _gather(hl, pose_l, affe_l, DEC_BATCH, DEC_SEQ, eg=4)
        yc, yl = _experts(xs_c, xs_l, val_c, val_l, w_gate, w_up, w_down, l)
        xc = _combine(xc, mod3, post_c, yc, BATCH, SEQ, gate_chunk=5, mod_base=0, per_req=False)
        xl = _combine(xl, mod3, post_l, yl, DEC_BATCH, DEC_SEQ, gate_chunk=5, mod_base=1, per_req=True)
    y_prompt = _normmod(xc, final_norm_g, out_dtype=F32, name="final_norm_ctx").reshape(BATCH, SEQ, D_MODEL)
    y_sample = _normmod(xl, final_norm_g, out_dtype=F32, name="final_norm_lat").reshape(DEC_BATCH, DEC_SEQ, D_MODEL)
    kv_na, kv_diff, kv_win, kv_gqa = (jnp.stack(lst, axis=1) for lst in kvs)
    return (y_prompt, y_sample, kv_na, kv_diff, kv_win, kv_gqa)
```
